```python
import jax, jax.numpy as jnp
from jax import lax
import numpy as np

D_MODEL = 1024
BATCH = 2
SEQ = 16384
DEPTH = 2

D_NSA = 512
D_LRU = 512
D_CONV = 512
D_MIX = D_NSA + D_LRU + D_CONV
HEAD_DIM = 64
N_HEADS = D_NSA // HEAD_DIM
N_KV = 2
N_REP = N_HEADS // N_KV
KV_W = N_KV * HEAD_DIM
L_CMP = 32
S_CMP = 16
CMP_HIDDEN = 256
L_SLC = 64
K_SEL = 16
WINDOW = 512
Q_BLOCK = 128
N_BRANCH = 3
ROPE_THETA = 10000.0
LRU_BLOCKS = 8
LRU_BLOCK_W = D_LRU // LRU_BLOCKS
LRU_CONV = 4
LRU_C = 8.0
CONV_K = 31
EPS = 1e-6
NEG = -1e30
FORCE = 1e4
COL_SIZES = (D_NSA, KV_W, KV_W, KV_W, KV_W, KV_W, KV_W, N_HEADS * N_BRANCH,
             D_NSA, D_LRU, D_LRU, D_CONV, D_CONV, D_CONV)
N_IN = D_NSA * 2 + KV_W * 6 + N_HEADS * N_BRANCH + D_LRU * 2 + D_CONV * 3

kernel_name = "hybrid_nsa_rglru_conformer_parallel"


def split_points():
    return [int(v) for v in np.cumsum(COL_SIZES)[:-1]]


def rms_norm(x, g):
    xf = x.astype(jnp.float32)
    y = xf * lax.rsqrt(jnp.mean(xf * xf, axis=-1, keepdims=True) + EPS)
    return (y * g.astype(jnp.float32)).astype(x.dtype)


def rope_tables(positions):
    inv_freq = ROPE_THETA ** (-jnp.arange(0, HEAD_DIM, 2, dtype=jnp.float32) / HEAD_DIM)
    ang = positions.astype(jnp.float32)[..., None] * inv_freq
    return jnp.cos(ang)[:, :, None, :], jnp.sin(ang)[:, :, None, :]


def apply_rope(u, cos, sin):
    uf = u.astype(jnp.float32)
    u1, u2 = jnp.split(uf, 2, axis=-1)
    out = jnp.concatenate([u1 * cos - u2 * sin, u2 * cos + u1 * sin], axis=-1)
    return out.astype(u.dtype)


def causal_dwconv(u, w, b):
    k = w.shape[0]
    c = u.shape[-1]
    y = lax.conv_general_dilated(u, w[:, None, :].astype(u.dtype), window_strides=(1,),
                                 padding=[(k - 1, 0)], dimension_numbers=('NWC', 'WIO', 'NWC'),
                                 feature_group_count=c)
    return y + b


def nsa_attention(q, k_c, v_c, k_s, v_s, k_w, v_w, gates, cmp_pos, cmp_w1, cmp_w2):
    B, T = q.shape[0], q.shape[1]
    qg = q.reshape(B, T, N_KV, N_REP, HEAD_DIM).transpose(0, 2, 3, 1, 4)
    tr = lambda a: a.transpose(0, 2, 1, 3)
    k_c, v_c, k_s, v_s, k_w, v_w = tr(k_c), tr(v_c), tr(k_s), tr(v_s), tr(k_w), tr(v_w)

    n_cmp = (T - L_CMP) // S_CMP + 1
    cmp_start = jnp.arange(n_cmp) * S_CMP
    idx = cmp_start[:, None] + jnp.arange(L_CMP)[None, :]

    def compress(kv, pos, w1, w2):
        blk = (kv[:, :, idx] + pos).reshape(B, N_KV, n_cmp, L_CMP * HEAD_DIM)
        return jax.nn.silu(blk @ w1) @ w2

    kc = compress(k_c, cmp_pos[0], cmp_w1[0], cmp_w2[0])
    vc = compress(v_c, cmp_pos[1], cmp_w1[1], cmp_w2[1])
    cmp_end = cmp_start + L_CMP - 1

    n_sel = T // L_SLC
    k_sel = min(K_SEL, n_sel)
    sel_start = jnp.arange(n_sel) * L_SLC
    overlap = ((cmp_start[:, None] < sel_start[None, :] + L_SLC) &
               (cmp_start[:, None] + L_CMP > sel_start[None, :])).astype(jnp.float32)
    ks_blk = k_s.reshape(B, N_KV, n_sel, L_SLC, HEAD_DIM)
    vs_blk = v_s.reshape(B, N_KV, n_sel, L_SLC, HEAD_DIM)
    bi = jnp.arange(B)[:, None, None, None]
    gi = jnp.arange(N_KV)[None, :, None, None]
    sel_j = jnp.arange(n_sel)

    pad = ((0, 0), (0, 0), (WINDOW, 0), (0, 0))
    kw_pad = jnp.pad(k_w, pad)
    vw_pad = jnp.pad(v_w, pad)
    scale = HEAD_DIM ** -0.5

    def block(i):
        s = i * Q_BLOCK
        qb = lax.dynamic_slice_in_dim(qg, s, Q_BLOCK, axis=3)
        t = s + jnp.arange(Q_BLOCK)
        sc = jnp.einsum('bgrqd,bgnd->bgrqn', qb, kc).astype(jnp.float32) * scale
        mc = cmp_end[None, :] <= t[:, None]
        pc = jnp.where(mc, jax.nn.softmax(jnp.where(mc, sc, NEG), axis=-1), 0.0)
        o_c = jnp.einsum('bgrqn,bgnd->bgrqd', pc.astype(vc.dtype), vc)
        imp = jnp.einsum('bgrqn,ns->bgqs', pc, overlap)
        cur = t // L_SLC
        forced = ((sel_j[None, :] == 0) | (sel_j[None, :] == cur[:, None]) |
                  (sel_j[None, :] == cur[:, None] - 1))
        valid = sel_start[None, :] <= t[:, None]
        score = jnp.where(forced, FORCE, jnp.where(valid, imp, -1.0))
        top_v, top_i = lax.top_k(score, k_sel)
        ok = top_v >= 0.0
        kg = ks_blk[bi, gi, top_i]
        vg = vs_blk[bi, gi, top_i]
        ss = jnp.einsum('bgrqd,bgqkld->bgrqkl', qb, kg).astype(jnp.float32) * scale
        tok = top_i[..., None] * L_SLC + jnp.arange(L_SLC)
        ms = (ok[..., None] & (tok <= t[None, None, :, None, None]))[:, :, None]
        ss = jnp.where(ms, ss, NEG).reshape(B, N_KV, N_REP, Q_BLOCK, k_sel * L_SLC)
        ps = jax.nn.softmax(ss, axis=-1).reshape(B, N_KV, N_REP, Q_BLOCK, k_sel, L_SLC)
        o_s = jnp.einsum('bgrqkl,bgqkld->bgrqd', ps.astype(vg.dtype), vg)
        kw = lax.dynamic_slice_in_dim(kw_pad, s, WINDOW + Q_BLOCK, axis=2)
        vw = lax.dynamic_slice_in_dim(vw_pad, s, WINDOW + Q_BLOCK, axis=2)
        pos = s - WINDOW + jnp.arange(WINDOW + Q_BLOCK)
        mw = ((pos[None, :] <= t[:, None]) & (pos[None, :] > t[:, None] - WINDOW) &
              (pos[None, :] >= 0))
        sw = jnp.einsum('bgrqd,bgkd->bgrqk', qb, kw).astype(jnp.float32) * scale
        pw = jax.nn.softmax(jnp.where(mw, sw, NEG), axis=-1)
        o_w = jnp.einsum('bgrqk,bgkd->bgrqd', pw.astype(vw.dtype), vw)
        return o_c, o_s, o_w

    o_c, o_s, o_w = lax.map(block, jnp.arange(T // Q_BLOCK))

    def unblock(o):
        return o.transpose(1, 0, 4, 2, 3, 5).reshape(B, T, N_KV, N_REP, HEAD_DIM)

    g = gates.reshape(B, T, N_KV, N_REP, N_BRANCH)
    out = g[..., 0:1] * unblock(o_c) + g[..., 1:2] * unblock(o_s) + g[..., 2:3] * unblock(o_w)
    return out.reshape(B, T, D_NSA)


def rg_lru(u, conv_w, conv_b, w_a, b_a, w_i, b_i, lam):
    B, T, C = u.shape
    xc = causal_dwconv(u, conv_w, conv_b)
    xh = xc.reshape(B, T, LRU_BLOCKS, LRU_BLOCK_W)
    r = jax.nn.sigmoid(jnp.einsum('bthi,hij->bthj', xh, w_a).reshape(B, T, C) + b_a)
    ig = jax.nn.sigmoid(jnp.einsum('bthi,hij->bthj', xh, w_i).reshape(B, T, C) + b_i)
    log_a = -LRU_C * r.astype(jnp.float32) * jax.nn.softplus(-lam.astype(jnp.float32))
    a = jnp.exp(log_a)
    bterm = jnp.sqrt(-jnp.expm1(2.0 * log_a)) * (ig * xc).astype(jnp.float32)

    def combine(left, right):
        a1, b1 = left
        a2, b2 = right
        return a1 * a2, a2 * b1 + b2

    _, h = lax.associative_scan(combine, (a, bterm), axis=1)
    return h.astype(u.dtype)


def conformer_conv(ga, gb, dw_w, dw_b, ln_g, ln_b, w_pw2):
    u = ga * jax.nn.sigmoid(gb)
    u = causal_dwconv(u, dw_w, dw_b)
    uf = u.astype(jnp.float32)
    mu = jnp.mean(uf, axis=-1, keepdims=True)
    var = jnp.mean(jnp.square(uf - mu), axis=-1, keepdims=True)
    un = (uf - mu) * lax.rsqrt(var + EPS) * ln_g.astype(jnp.float32) + ln_b.astype(jnp.float32)
    return jax.nn.silu(un).astype(u.dtype) @ w_pw2


def setup_inputs(seed: int = 0) -> dict:
    key = jax.random.key(seed)
    ks = jax.random.split(key, 24)
    f32 = jnp.float32
    nrm = lambda k, shape, s: jax.random.normal(k, shape, f32) * s
    u = jax.random.uniform(ks[13], (DEPTH, D_LRU), f32, 0.9, 0.999)
    sg = u ** (1.0 / LRU_C)
    lam = jnp.log(sg / (1.0 - sg))
    return {
        "x": jax.random.normal(ks[0], (BATCH, SEQ, D_MODEL), f32),
        "positions": jnp.broadcast_to(jnp.arange(SEQ, dtype=jnp.int32), (BATCH, SEQ)),
        "norm_g": 1.0 + nrm(ks[1], (DEPTH, D_MODEL), 0.02),
        "w_in": nrm(ks[2], (DEPTH, D_MODEL, N_IN), D_MODEL ** -0.5),
        "b_gate": nrm(ks[3], (DEPTH, N_HEADS * N_BRANCH), 0.02),
        "cmp_pos": nrm(ks[4], (DEPTH, 2, L_CMP, HEAD_DIM), 0.02),
        "cmp_w1": nrm(ks[5], (DEPTH, 2, L_CMP * HEAD_DIM, CMP_HIDDEN), (L_CMP * HEAD_DIM) ** -0.5),
        "cmp_w2": nrm(ks[6], (DEPTH, 2, CMP_HIDDEN, HEAD_DIM), CMP_HIDDEN ** -0.5),
        "lru_conv_w": nrm(ks[7], (DEPTH, LRU_CONV, D_LRU), LRU_CONV ** -0.5),
        "lru_conv_b": nrm(ks[8], (DEPTH, D_LRU), 0.02),
        "lru_w_a": nrm(ks[9], (DEPTH, LRU_BLOCKS, LRU_BLOCK_W, LRU_BLOCK_W), LRU_BLOCK_W ** -0.5),
        "lru_b_a": nrm(ks[10], (DEPTH, D_LRU), 0.02),
        "lru_w_i": nrm(ks[11], (DEPTH, LRU_BLOCKS, LRU_BLOCK_W, LRU_BLOCK_W), LRU_BLOCK_W ** -0.5),
        "lru_b_i": nrm(ks[12], (DEPTH, D_LRU), 0.02),
        "lru_lam": lam,
        "conv_dw_w": nrm(ks[14], (DEPTH, CONV_K, D_CONV), CONV_K ** -0.5),
        "conv_dw_b": nrm(ks[15], (DEPTH, D_CONV), 0.02),
        "conv_ln_g": 1.0 + nrm(ks[16], (DEPTH, D_CONV), 0.02),
        "conv_ln_b": nrm(ks[17], (DEPTH, D_CONV), 0.02),
        "conv_w_pw2": nrm(ks[18], (DEPTH, D_CONV, D_CONV), D_CONV ** -0.5),
        "w_out": nrm(ks[19], (DEPTH, D_MIX, D_MODEL), D_MIX ** -0.5),
        "final_g": 1.0 + nrm(ks[20], (D_MODEL,), 0.02),
    }


def reference(x, positions, norm_g, w_in, b_gate, cmp_pos, cmp_w1, cmp_w2, lru_conv_w, lru_conv_b,
              lru_w_a, lru_b_a, lru_w_i, lru_b_i, lru_lam, conv_dw_w, conv_dw_b, conv_ln_g, conv_ln_b,
              conv_w_pw2, w_out, final_g):
    B, T = x.shape[0], x.shape[1]
    cos, sin = rope_tables(positions)
    for l in range(DEPTH):
        h = rms_norm(x, norm_g[l])
        proj = h @ w_in[l]
        (q, k_c, v_c, k_s, v_s, k_w, v_w, g_lin, z_nsa, u_lru, z_lru, glu_a, glu_b,
         z_conv) = jnp.split(proj, split_points(), axis=-1)
        kvr = lambda a: a.reshape(B, T, N_KV, HEAD_DIM)
        q = apply_rope(q.reshape(B, T, N_HEADS, HEAD_DIM), cos, sin)
        k_c = apply_rope(kvr(k_c), cos, sin)
        k_s = apply_rope(kvr(k_s), cos, sin)
        k_w = apply_rope(kvr(k_w), cos, sin)
        gates = jax.nn.sigmoid(g_lin + b_gate[l]).reshape(B, T, N_HEADS, N_BRANCH)
        y_nsa = nsa_attention(q, k_c, kvr(v_c), k_s, kvr(v_s), k_w, kvr(v_w), gates,
                              cmp_pos[l], cmp_w1[l], cmp_w2[l])
        y_lru = rg_lru(u_lru, lru_conv_w[l], lru_conv_b[l], lru_w_a[l], lru_b_a[l],
                       lru_w_i[l], lru_b_i[l], lru_lam[l])
        y_conv = conformer_conv(glu_a, glu_b, conv_dw_w[l], conv_dw_b[l], conv_ln_g[l],
                                conv_ln_b[l], conv_w_pw2[l])
        mixed = jnp.concatenate([y_nsa * jax.nn.silu(z_nsa), y_lru * jax.nn.silu(z_lru),
                                 y_conv * jax.nn.silu(z_conv)], axis=-1)
        x = x + mixed @ w_out[l]
    return rms_norm(x, final_g)
```

```python
import functools
import math

import numpy as np
import jax
import jax.numpy as jnp
from jax import lax
from jax.experimental import pallas as pl
from jax.experimental.pallas import tpu as pltpu

D_MODEL = 1024
D_GRP = 512
HEAD_DIM = 64
N_HEADS = 8
N_KV = 2
N_REP = N_HEADS // N_KV
N_BRANCH = 3
L_CMP = 32
S_CMP = 16
CMP_HIDDEN = 256
L_SLC = 64
K_SEL = 16
WINDOW = 512
ROPE_THETA = 10000.0
LRU_BLOCKS = 8
LRU_CONV = 4
LRU_C = 8.0
CONV_K = 31
EPS = 1e-6
NEG = -1e30
FORCE = 1e4

LANES = 128
SUBLANES = 8
TM = 512
TQ = 256
TK = 512
BLK_PER_TK = TK // L_SLC
GATE_ROWS = 16
N_NAT = 4 * LANES + 6 * D_GRP
N_TRN = D_GRP + 2 * LANES + 2 * GATE_ROWS
QSCALE = HEAD_DIM ** -0.5 * math.log2(math.e)
VMEM_LIMIT = 56 * 1024 * 1024
PICKED = -3e38

F32 = jnp.float32
BF16 = jnp.bfloat16


def _dot(a, b):
    return jnp.dot(a, b, preferred_element_type=F32)


def _silu(v):
    return v * jax.nn.sigmoid(v)


def _params(*sem):
    return pltpu.CompilerParams(dimension_semantics=sem, vmem_limit_bytes=VMEM_LIMIT)


def _inproj_kernel(x_ref, g_ref, wn_ref, wt_ref, posc_ref, posr_ref, invn_ref, invt_ref, bg_ref,
                   kvc_ref, ksa_ref, kwa_ref, zn_ref, ul_ref, zl_ref, ga_ref, gb_ref, zc_ref,
                   qt_ref, vst_ref, vwt_ref, gt_ref):
    ti = pl.program_id(1)
    x = x_ref[...]
    r = lax.rsqrt(jnp.mean(x * x, axis=-1, keepdims=True) + EPS)
    hn = (x * r * g_ref[...]).astype(BF16)

    pk = _dot(hn, wn_ref[:, 0:4 * LANES])
    ang = posc_ref[...] * invn_ref[...]
    cosn = jnp.cos(ang)
    sinn = jnp.sin(ang)
    lane = lax.broadcasted_iota(jnp.int32, (TM, LANES), 1)
    row = lax.broadcasted_iota(jnp.int32, (TM, LANES), 0)
    first = (lane % HEAD_DIM) < (HEAD_DIM // 2)
    sins = jnp.where(first, -sinn, sinn)

    def rope_n(u):
        partner = jnp.where(first, pltpu.roll(u, LANES - HEAD_DIM // 2, 1), pltpu.roll(u, HEAD_DIM // 2, 1))
        return u * cosn + partner * sins

    kc = rope_n(pk[:, 0:LANES])
    ks = rope_n(pk[:, LANES:2 * LANES])
    kw = rope_n(pk[:, 2 * LANES:3 * LANES])
    kvc_ref[0] = kc
    kvc_ref[1] = pk[:, 3 * LANES:4 * LANES]
    blk = ((ti * TM + row) // L_SLC) % BLK_PER_TK
    onehot = jnp.where((lane - HEAD_DIM) == blk, 1.0, 0.0)
    lo = lane < HEAD_DIM
    ks1 = pltpu.roll(ks, HEAD_DIM, 1)
    kw1 = pltpu.roll(kw, HEAD_DIM, 1)
    ksa_ref[0] = jnp.where(lo, ks, onehot).astype(BF16)
    ksa_ref[1] = jnp.where(lo, ks1, onehot).astype(BF16)
    kwa_ref[0] = jnp.where(lo, kw, 0.0).astype(BF16)
    kwa_ref[1] = jnp.where(lo, kw1, 0.0).astype(BF16)
    for i, ref in enumerate((zn_ref, ul_ref, zl_ref, ga_ref, gb_ref, zc_ref)):
        c0 = 4 * LANES + i * D_GRP
        ref[...] = _dot(hn, wn_ref[:, c0:c0 + D_GRP])

    pt = lax.dot_general(wt_ref[...], hn, (((1,), (1,)), ((), ())), preferred_element_type=F32)
    angt = invt_ref[...] * posr_ref[...]
    cost = jnp.cos(angt)
    sint = jnp.sin(angt)
    half = HEAD_DIM // 2
    for h in range(N_HEADS):
        u1 = pt[h * HEAD_DIM:h * HEAD_DIM + half]
        u2 = pt[h * HEAD_DIM + half:(h + 1) * HEAD_DIM]
        qt_ref[h * HEAD_DIM:h * HEAD_DIM + half, :] = ((u1 * cost - u2 * sint) * QSCALE).astype(BF16)
        qt_ref[h * HEAD_DIM + half:(h + 1) * HEAD_DIM, :] = ((u2 * cost + u1 * sint) * QSCALE).astype(BF16)
    vst_ref[...] = pt[D_GRP:D_GRP + LANES].astype(BF16)
    vwt_ref[...] = pt[D_GRP + LANES:D_GRP + 2 * LANES].astype(BF16)
    g0 = D_GRP + 2 * LANES
    gt_ref[...] = jax.nn.sigmoid(pt[g0:g0 + 2 * GATE_ROWS] + bg_ref[...])


def _inproj(x, norm_g, wn, wt, posc, posr, invn, invt, bg):
    B, T, _ = x.shape
    nt = T // TM
    const = lambda shape: pl.BlockSpec(shape, lambda b, t: (0,) * len(shape))
    nat = jax.ShapeDtypeStruct((B, T, D_GRP), F32)
    nat_spec = pl.BlockSpec((None, TM, D_GRP), lambda b, t: (b, t, 0))
    return pl.pallas_call(
        _inproj_kernel,
        grid=(B, nt),
        in_specs=[
            pl.BlockSpec((None, TM, D_MODEL), lambda b, t: (b, t, 0)),
            const((1, D_MODEL)),
            const((D_MODEL, N_NAT)),
            const((N_TRN, D_MODEL)),
            pl.BlockSpec((None, TM, 1), lambda b, t: (b, t, 0)),
            pl.BlockSpec((None, 1, TM), lambda b, t: (b, 0, t)),
            const((1, LANES)),
            const((HEAD_DIM // 2, 1)),
            const((2 * GATE_ROWS, 1)),
        ],
        out_specs=[
            pl.BlockSpec((2, None, TM, LANES), lambda b, t: (0, b, t, 0)),
            pl.BlockSpec((None, N_KV, TM, LANES), lambda b, t: (b, 0, t, 0)),
            pl.BlockSpec((None, N_KV, TM, LANES), lambda b, t: (b, 0, t, 0)),
            nat_spec, nat_spec, nat_spec, nat_spec, nat_spec, nat_spec,
            pl.BlockSpec((None, D_GRP, TM), lambda b, t: (b, 0, t)),
            pl.BlockSpec((None, None, LANES, TM), lambda b, t: (b, t, 0, 0)),
            pl.BlockSpec((None, None, LANES, TM), lambda b, t: (b, t, 0, 0)),
            pl.BlockSpec((None, 2 * GATE_ROWS, TM), lambda b, t: (b, 0, t)),
        ],
        out_shape=[
            jax.ShapeDtypeStruct((2, B, T, LANES), F32),
            jax.ShapeDtypeStruct((B, N_KV, T, LANES), BF16),
            jax.ShapeDtypeStruct((B, N_KV, T, LANES), BF16),
            nat, nat, nat, nat, nat, nat,
            jax.ShapeDtypeStruct((B, D_GRP, T), BF16),
            jax.ShapeDtypeStruct((B, nt, LANES, TM), BF16),
            jax.ShapeDtypeStruct((B, nt, LANES, TM), BF16),
            jax.ShapeDtypeStruct((B, 2 * GATE_ROWS, T), F32),
        ],
        compiler_params=_params("parallel", "parallel"),
        name="inproj",
    )(x, norm_g, wn, wt, posc, posr, invn, invt, bg)


def _compress_kernel(x_ref, pa_ref, pb_ref, wtop_ref, wbot_ref, w2_ref, nat_ref, tr_ref):
    nc = x_ref.shape[0]
    x = x_ref[...]
    a = _dot((x + pa_ref[...]).astype(BF16), wtop_ref[...])
    b = _dot((x + pb_ref[...]).astype(BF16), wbot_ref[...])
    hid = _silu(a + pltpu.roll(b, nc - 1, 0)).astype(BF16)
    out = _dot(hid, w2_ref[...])
    rowi = lax.broadcasted_iota(jnp.int32, out.shape, 0)
    out = jnp.where(rowi < nc - 1, out, 0.0)
    nat_ref[0] = out[:, 0:HEAD_DIM].astype(BF16)
    nat_ref[1] = out[:, HEAD_DIM:2 * HEAD_DIM].astype(BF16)
    tr_ref[...] = out.T.astype(BF16)


def _compress(kvc2, posa, posb, wtop, wbot, w2):
    _, B, nc, width = kvc2.shape
    return pl.pallas_call(
        _compress_kernel,
        grid=(2, B),
        in_specs=[
            pl.BlockSpec((None, None, nc, width), lambda k, b: (k, b, 0, 0)),
            pl.BlockSpec((None, 1, width), lambda k, b: (k, 0, 0)),
            pl.BlockSpec((None, 1, width), lambda k, b: (k, 0, 0)),
            pl.BlockSpec((None, width, N_KV * CMP_HIDDEN), lambda k, b: (k, 0, 0)),
            pl.BlockSpec((None, width, N_KV * CMP_HIDDEN), lambda k, b: (k, 0, 0)),
            pl.BlockSpec((None, N_KV * CMP_HIDDEN, LANES), lambda k, b: (k, 0, 0)),
        ],
        out_specs=[
            pl.BlockSpec((None, None, N_KV, nc, HEAD_DIM), lambda k, b: (k, b, 0, 0, 0)),
            pl.BlockSpec((None, None, LANES, nc), lambda k, b: (k, b, 0, 0)),
        ],
        out_shape=[
            jax.ShapeDtypeStruct((2, B, N_KV, nc, HEAD_DIM), BF16),
            jax.ShapeDtypeStruct((2, B, LANES, nc), BF16),
        ],
        compiler_params=_params("parallel", "parallel"),
        name="compress",
    )(kvc2, posa, posb, wtop, wbot, w2)


def _cmp_kernel(qt_ref, kc_ref, vct_ref, gt_ref, ovt_ref, yct_ref, sb_ref):
    nc = kc_ref.shape[0]
    ns = sb_ref.shape[0]
    qi = pl.program_id(2)
    t = qi * TQ + lax.broadcasted_iota(jnp.int32, (1, TQ), 1)
    j = lax.broadcasted_iota(jnp.int32, (nc, 1), 0)
    mc = (S_CMP * j + (L_CMP - 1)) <= t
    kc = kc_ref[...]
    vct = vct_ref[...]
    psum = jnp.zeros((nc, TQ), F32)
    for r in range(N_REP):
        s = _dot(kc, qt_ref[r * HEAD_DIM:(r + 1) * HEAD_DIM, :])
        s = jnp.where(mc, s, NEG)
        m = jnp.max(s, axis=0, keepdims=True)
        p = jnp.where(mc, jnp.exp2(s - m), 0.0)
        l = jnp.sum(p, axis=0, keepdims=True)
        has = l > 0.0
        pc = p * jnp.where(has, 1.0 / jnp.where(has, l, 1.0), 0.0)
        oc = _dot(vct, pc.astype(BF16))
        yct_ref[r * HEAD_DIM:(r + 1) * HEAD_DIM, :] = oc * gt_ref[r * N_BRANCH:r * N_BRANCH + 1, :]
        psum = psum + pc

    hi = psum.astype(BF16)
    r1 = psum - hi.astype(F32)
    mid = r1.astype(BF16)
    low = (r1 - mid.astype(F32)).astype(BF16)
    ovt = ovt_ref[...]
    imp = _dot(ovt, hi) + _dot(ovt, mid) + _dot(ovt, low)

    sidx = lax.broadcasted_iota(jnp.int32, (ns, TQ), 0)
    cur = t // L_SLC
    forced = (sidx == 0) | (sidx == cur) | (sidx == cur - 1)
    valid = sidx * L_SLC <= t
    score = jnp.where(forced, FORCE, jnp.where(valid, imp, -1.0))

    def take_max(_, c):
        mx = jnp.max(c, axis=0, keepdims=True)
        idx = jnp.min(jnp.where(c == mx, sidx, ns), axis=0, keepdims=True)
        return jnp.where(sidx == idx, PICKED, c)

    c = lax.fori_loop(0, min(K_SEL, ns), take_max, score)
    sel = (c == PICKED) & (score >= 0.0)
    sb_ref[...] = jnp.where(sel, 0.0, NEG)


def _cmp_attention(qt, cnat, ctr, gt, ovt):
    B, _, T = qt.shape
    nc = cnat.shape[3]
    ns = ovt.shape[0]
    nq = T // TQ
    return pl.pallas_call(
        _cmp_kernel,
        grid=(B, N_KV, nq),
        in_specs=[
            pl.BlockSpec((None, N_REP * HEAD_DIM, TQ), lambda b, g, q: (b, g, q)),
            pl.BlockSpec((None, None, None, nc, HEAD_DIM), lambda b, g, q: (0, b, g, 0, 0)),
            pl.BlockSpec((None, None, HEAD_DIM, nc), lambda b, g, q: (1, b, g, 0)),
            pl.BlockSpec((None, None, GATE_ROWS, TQ), lambda b, g, q: (b, g, 0, q)),
            pl.BlockSpec((ns, nc), lambda b, g, q: (0, 0)),
        ],
        out_specs=[
            pl.BlockSpec((None, N_REP * HEAD_DIM, TQ), lambda b, g, q: (b, g, q)),
            pl.BlockSpec((None, None, ns, TQ), lambda b, g, q: (b, g, 0, q)),
        ],
        out_shape=[
            jax.ShapeDtypeStruct((B, D_GRP, T), F32),
            jax.ShapeDtypeStruct((B, N_KV, ns, T), F32),
        ],
        compiler_params=_params("parallel", "parallel", "parallel"),
        name="cmp_attention",
    )(qt, cnat, ctr, gt, ovt)


def _sel_kernel(qt_ref, ksa_ref, vst_ref, kw0_ref, kw1_ref, vw0_ref, vw1_ref, sb_ref, yct_ref, gt_ref,
                zn_ref, out_ref, w_ref, m_ref, l_ref, acc_ref, y_ref):
    qi = pl.program_id(2)
    d = (qi * TQ) // TK
    t = qi * TQ + lax.broadcasted_iota(jnp.int32, (1, TQ), 1)
    krow = lax.broadcasted_iota(jnp.int32, (TK, 1), 0)
    rows = N_REP * HEAD_DIM

    for r in range(N_REP):
        w_ref[r, 0:HEAD_DIM, :] = qt_ref[r * HEAD_DIM:(r + 1) * HEAD_DIM, :]
        w_ref[r, HEAD_DIM:LANES, :] = jnp.zeros((LANES - HEAD_DIM, TQ), BF16)

    def reset():
        m_ref[...] = jnp.full(m_ref.shape, NEG, F32)
        l_ref[...] = jnp.zeros(l_ref.shape, F32)
        acc_ref[...] = jnp.zeros(acc_ref.shape, F32)

    def tile(ktile, vtile, mask):
        for r in range(N_REP):
            s = _dot(ktile, w_ref[r])
            if mask is not None:
                s = jnp.where(mask, s, NEG)
            m_old = m_ref[r:r + 1, :]
            m_new = jnp.maximum(m_old, jnp.max(s, axis=0, keepdims=True))
            alpha = jnp.exp2(m_old - m_new)
            p = jnp.exp2(s - m_new)
            l_ref[r:r + 1, :] = alpha * l_ref[r:r + 1, :] + jnp.sum(p, axis=0, keepdims=True)
            acc_ref[r] = alpha * acc_ref[r] + _dot(vtile, p.astype(BF16))
            m_ref[r:r + 1, :] = m_new

    def set_bias(kt):
        bias = sb_ref[pl.ds(pl.multiple_of(kt * BLK_PER_TK, BLK_PER_TK), BLK_PER_TK), :]
        b16 = jnp.concatenate([bias, jnp.zeros_like(bias)], axis=0).astype(BF16)
        for r in range(N_REP):
            w_ref[r, HEAD_DIM:HEAD_DIM + 2 * BLK_PER_TK, :] = b16

    def gated(branch):
        for r in range(N_REP):
            o = acc_ref[r] / l_ref[r:r + 1, :]
            g = gt_ref[r * N_BRANCH + branch:r * N_BRANCH + branch + 1, :]
            y_ref[r * HEAD_DIM:(r + 1) * HEAD_DIM, :] += o * g

    y_ref[...] = yct_ref[...]

    reset()

    def body(kt, carry):
        set_bias(kt)
        k0 = pl.multiple_of(kt * TK, TK)
        tile(ksa_ref[pl.ds(k0, TK), :], vst_ref[kt], None)
        return carry

    lax.fori_loop(0, d, body, 0)
    set_bias(d)
    kd = pl.multiple_of(d * TK, TK)
    pos_d = d * TK + krow
    tile(ksa_ref[pl.ds(kd, TK), :], vst_ref[d], pos_d <= t)
    gated(1)

    reset()
    tile(kw1_ref[...], vw1_ref[...], (pos_d <= t) & (pos_d > t - WINDOW))
    pos_p = pos_d - TK
    tile(kw0_ref[...], vw0_ref[...], (pos_p >= 0) & (pos_p > t - WINDOW))
    gated(2)

    out_ref[...] = (y_ref[...].T * _silu(zn_ref[...])).astype(BF16)


def _sel_attention(qt, ksa, vst, kwa, vwt, sb, yct, gt, zn):
    B, _, T = qt.shape
    ns = sb.shape[2]
    nq = T // TQ
    nkt = T // TK
    rows = N_REP * HEAD_DIM
    dtile = lambda q: (q * TQ) // TK
    ptile = lambda q: jnp.maximum(dtile(q) - 1, 0)
    return pl.pallas_call(
        _sel_kernel,
        grid=(B, N_KV, nq),
        in_specs=[
            pl.BlockSpec((None, rows, TQ), lambda b, g, q: (b, g, q)),
            pl.BlockSpec((None, None, T, LANES), lambda b, g, q: (b, g, 0, 0)),
            pl.BlockSpec((None, nkt, HEAD_DIM, TK), lambda b, g, q: (b, 0, g, 0)),
            pl.BlockSpec((None, None, TK, LANES), lambda b, g, q: (b, g, ptile(q), 0)),
            pl.BlockSpec((None, None, TK, LANES), lambda b, g, q: (b, g, dtile(q), 0)),
            pl.BlockSpec((None, None, HEAD_DIM, TK), lambda b, g, q: (b, ptile(q), g, 0)),
            pl.BlockSpec((None, None, HEAD_DIM, TK), lambda b, g, q: (b, dtile(q), g, 0)),
            pl.BlockSpec((None, None, ns, TQ), lambda b, g, q: (b, g, 0, q)),
            pl.BlockSpec((None, rows, TQ), lambda b, g, q: (b, g, q)),
            pl.BlockSpec((None, None, GATE_ROWS, TQ), lambda b, g, q: (b, g, 0, q)),
            pl.BlockSpec((None, TQ, rows), lambda b, g, q: (b, q, g)),
        ],
        out_specs=pl.BlockSpec((None, TQ, rows), lambda b, g, q: (b, q, g)),
        out_shape=jax.ShapeDtypeStruct((B, T, D_GRP), BF16),
        scratch_shapes=[
            pltpu.VMEM((N_REP, LANES, TQ), BF16),
            pltpu.VMEM((SUBLANES, TQ), F32),
            pltpu.VMEM((SUBLANES, TQ), F32),
            pltpu.VMEM((N_REP, HEAD_DIM, TQ), F32),
            pltpu.VMEM((rows, TQ), F32),
        ],
        compiler_params=_params("parallel", "parallel", "arbitrary"),
        name="sel_attention",
    )(qt, ksa, vst, kwa, kwa, vwt, vwt, sb, yct, gt, zn)


LRU_TT = 512
CHUNK = 32


def _lru_kernel(u_ref, z_ref, cw_ref, cb_ref, wg_ref, ba_ref, bi_ref, lam_ref, out_ref,
                ubuf, a_s, b_s, hcar):
    ti = pl.program_id(1)
    tt = LRU_TT

    @pl.when(ti == 0)
    def _():
        ubuf[0:SUBLANES, :] = jnp.zeros((SUBLANES, D_GRP), F32)
        hcar[...] = jnp.zeros(hcar.shape, F32)

    u = u_ref[...]
    ubuf[SUBLANES:SUBLANES + tt, :] = u
    xc = jnp.broadcast_to(cb_ref[...], (tt, D_GRP))
    for k in range(LRU_CONV):
        xc = xc + cw_ref[k:k + 1, :] * ubuf[pl.ds(SUBLANES - (LRU_CONV - 1) + k, tt), :]
    ubuf[0:SUBLANES, :] = u[tt - SUBLANES:tt, :]

    gl = _dot(xc.astype(BF16), wg_ref[...])
    r = jax.nn.sigmoid(gl[:, 0:D_GRP] + ba_ref[...])
    ig = jax.nn.sigmoid(gl[:, D_GRP:2 * D_GRP] + bi_ref[...])
    nlam = -lam_ref[...]
    softplus = jnp.maximum(nlam, 0.0) + jnp.log1p(jnp.exp(-jnp.abs(nlam)))
    log_a = -LRU_C * r * softplus
    a = jnp.exp(log_a)
    bt = jnp.sqrt(-jnp.tanh(log_a) * (a * a + 1.0)) * (ig * xc)

    row8 = lax.broadcasted_iota(jnp.int32, (tt, D_GRP), 0) % SUBLANES
    for dlt in (1, 2, 4):
        a_sh = pltpu.roll(a, dlt, 0)
        b_sh = pltpu.roll(bt, dlt, 0)
        keep = row8 >= dlt
        bt = jnp.where(keep, a * b_sh + bt, bt)
        a = jnp.where(keep, a * a_sh, a)
    a_s[...] = a
    b_s[...] = bt

    def step(gi, carry):
        r0 = pl.multiple_of(gi * SUBLANES, SUBLANES)
        blk = a_s[pl.ds(r0, SUBLANES), :] * carry + b_s[pl.ds(r0, SUBLANES), :]
        b_s[pl.ds(r0, SUBLANES), :] = blk
        return blk[SUBLANES - 1:SUBLANES, :]

    carry = lax.fori_loop(0, tt // SUBLANES, step, hcar[0:1, :])
    hcar[0:1, :] = carry
    out_ref[...] = (b_s[...] * _silu(z_ref[...])).astype(BF16)


def _lru(u, z, cw, cb, wg, ba, bi, lam):
    B, T, _ = u.shape
    tt = LRU_TT
    tile = pl.BlockSpec((None, tt, D_GRP), lambda b, t: (b, t, 0))
    const = lambda shape: pl.BlockSpec(shape, lambda b, t: (0,) * len(shape))
    return pl.pallas_call(
        _lru_kernel,
        grid=(B, T // tt),
        in_specs=[tile, tile, const((LRU_CONV, D_GRP)), const((1, D_GRP)), const((D_GRP, 2 * D_GRP)),
                  const((1, D_GRP)), const((1, D_GRP)), const((1, D_GRP))],
        out_specs=tile,
        out_shape=jax.ShapeDtypeStruct((B, T, D_GRP), BF16),
        scratch_shapes=[
            pltpu.VMEM((SUBLANES + tt, D_GRP), F32),
            pltpu.VMEM((tt, D_GRP), F32),
            pltpu.VMEM((tt, D_GRP), F32),
            pltpu.VMEM((SUBLANES, D_GRP), F32),
        ],
        compiler_params=_params("parallel", "arbitrary"),
        name="rg_lru",
    )(u, z, cw, cb, wg, ba, bi, lam)


CONV_TT = 512
CONV_HALO = 32


def _conv_kernel(ga_ref, gb_ref, z_ref, dw_ref, db_ref, lg_ref, lb_ref, wp_ref, out_ref, ubuf, sh_ref, c_s):
    ti = pl.program_id(1)
    tt = CONV_TT

    @pl.when(ti == 0)
    def _():
        ubuf[0:CONV_HALO, :] = jnp.zeros((CONV_HALO, D_GRP), F32)

    u = ga_ref[...] * jax.nn.sigmoid(gb_ref[...])
    ubuf[CONV_HALO:CONV_HALO + tt, :] = u
    span = tt + CONV_HALO - SUBLANES
    for sft in range(1, SUBLANES):
        sh_ref[sft - 1, 0:span, :] = ubuf[pl.ds(sft, span), :]

    def chunk(ci, carry):
        r0 = pl.multiple_of(ci * CHUNK, CHUNK)
        acc = jnp.broadcast_to(db_ref[...], (CHUNK, D_GRP))
        for k in range(CONV_K):
            hi, sft = divmod(CONV_HALO - (CONV_K - 1) + k, SUBLANES)
            src = ubuf if sft == 0 else sh_ref.at[sft - 1]
            acc = acc + dw_ref[k:k + 1, :] * src[pl.ds(pl.multiple_of(r0 + hi * SUBLANES, SUBLANES), CHUNK), :]
        c_s[pl.ds(r0, CHUNK), :] = acc
        return carry

    lax.fori_loop(0, tt // CHUNK, chunk, 0)
    ubuf[0:CONV_HALO, :] = u[tt - CONV_HALO:tt, :]

    c = c_s[...]
    mu = jnp.mean(c, axis=-1, keepdims=True)
    cen = c - mu
    var = jnp.mean(cen * cen, axis=-1, keepdims=True)
    un = cen * lax.rsqrt(var + EPS) * lg_ref[...] + lb_ref[...]
    y = _dot(_silu(un).astype(BF16), wp_ref[...])
    out_ref[...] = (y * _silu(z_ref[...])).astype(BF16)


def _conformer(ga, gb, z, dw, db, lg, lb, wp):
    B, T, _ = ga.shape
    tt = CONV_TT
    tile = pl.BlockSpec((None, tt, D_GRP), lambda b, t: (b, t, 0))
    const = lambda shape: pl.BlockSpec(shape, lambda b, t: (0,) * len(shape))
    return pl.pallas_call(
        _conv_kernel,
        grid=(B, T // tt),
        in_specs=[tile, tile, tile, const((CONV_HALO, D_GRP)), const((1, D_GRP)), const((1, D_GRP)),
                  const((1, D_GRP)), const((D_GRP, D_GRP))],
        out_specs=tile,
        out_shape=jax.ShapeDtypeStruct((B, T, D_GRP), BF16),
        scratch_shapes=[
            pltpu.VMEM((CONV_HALO + tt, D_GRP), F32),
            pltpu.VMEM((SUBLANES - 1, CONV_HALO + tt, D_GRP), F32),
            pltpu.VMEM((tt, D_GRP), F32),
        ],
        compiler_params=_params("parallel", "arbitrary"),
        name="conformer_conv",
    )(ga, gb, z, dw, db, lg, lb, wp)


def _outproj_kernel(x_ref, a_ref, b_ref, c_ref, w_ref, g_ref, o_ref, *, final):
    y = x_ref[...]
    y = y + _dot(a_ref[...], w_ref[0:D_GRP, :])
    y = y + _dot(b_ref[...], w_ref[D_GRP:2 * D_GRP, :])
    y = y + _dot(c_ref[...], w_ref[2 * D_GRP:3 * D_GRP, :])
    if final:
        y = y * lax.rsqrt(jnp.mean(y * y, axis=-1, keepdims=True) + EPS) * g_ref[...]
    o_ref[...] = y


def _outproj(x, ma, mb, mc, w, g, final):
    B, T, _ = x.shape
    xt = pl.BlockSpec((None, TM, D_MODEL), lambda b, t: (b, t, 0))
    mt = pl.BlockSpec((None, TM, D_GRP), lambda b, t: (b, t, 0))
    return pl.pallas_call(
        functools.partial(_outproj_kernel, final=final),
        grid=(B, T // TM),
        in_specs=[xt, mt, mt, mt,
                  pl.BlockSpec((3 * D_GRP, D_MODEL), lambda b, t: (0, 0)),
                  pl.BlockSpec((1, D_MODEL), lambda b, t: (0, 0))],
        out_specs=xt,
        out_shape=jax.ShapeDtypeStruct((B, T, D_MODEL), F32),
        compiler_params=_params("parallel", "parallel"),
        name="outproj",
    )(x, ma, mb, mc, w, g)


def _overlap_t(nc, ns):
    cs = np.arange(nc)[None, :] * S_CMP
    ss = np.arange(ns)[:, None] * L_SLC
    ov = (cs < ss + L_SLC) & (cs + L_CMP > ss) & (np.arange(nc)[None, :] < nc - 1)
    return jnp.asarray(ov.astype(np.float32), dtype=BF16)


def _layer_weights(l, w_in, b_gate, cmp_pos, cmp_w1, cmp_w2, lru_w_a, lru_w_i):
    w = w_in[l]
    c = np.cumsum((0, D_GRP, LANES, LANES, LANES, LANES, LANES, LANES, N_HEADS * N_BRANCH,
                   D_GRP, D_GRP, D_GRP, D_GRP, D_GRP, D_GRP))
    seg = lambda i: w[:, int(c[i]):int(c[i + 1])]
    q, k_c, v_c, k_s, v_s, k_w, v_w, g_lin = (seg(i) for i in range(8))
    wn = jnp.concatenate([k_c, k_s, k_w, v_c] + [seg(i) for i in range(8, 14)], axis=1).astype(BF16)
    per_grp = N_REP * N_BRANCH
    gpad = jnp.zeros((D_MODEL, GATE_ROWS - per_grp), F32)
    wt = jnp.concatenate([q, v_s, v_w, g_lin[:, :per_grp], gpad, g_lin[:, per_grp:], gpad], axis=1)
    wt = wt.T.astype(BF16)
    bpad = jnp.zeros((GATE_ROWS - per_grp,), F32)
    bg = jnp.concatenate([b_gate[l][:per_grp], bpad, b_gate[l][per_grp:], bpad])[:, None]

    eye = jnp.eye(N_KV, dtype=F32)
    half = L_CMP // 2
    w1 = cmp_w1[l].reshape(2, L_CMP, HEAD_DIM, CMP_HIDDEN)
    expand = lambda m: jnp.einsum('kldh,pg->klpdgh', m, eye).reshape(
        2, half * N_KV * HEAD_DIM, N_KV * CMP_HIDDEN).astype(BF16)
    wtop, wbot = expand(w1[:, :half]), expand(w1[:, half:])
    w2 = jnp.einsum('khd,pg->kphgd', cmp_w2[l], eye).reshape(2, N_KV * CMP_HIDDEN, LANES).astype(BF16)
    pos = cmp_pos[l]
    tilepos = lambda p: jnp.broadcast_to(p[:, :, None, :], (2, half, N_KV, HEAD_DIM)).reshape(2, 1, -1)
    posa, posb = tilepos(pos[:, :half]), tilepos(pos[:, half:])

    eye8 = jnp.eye(LRU_BLOCKS, dtype=F32)
    bd = lambda m: jnp.einsum('hij,hk->hikj', m, eye8).reshape(D_GRP, D_GRP)
    wg = jnp.concatenate([bd(lru_w_a[l]), bd(lru_w_i[l])], axis=1).astype(BF16)
    return wn, wt, bg, wtop, wbot, w2, posa, posb, wg


def kernel(x, positions, norm_g, w_in, b_gate, cmp_pos, cmp_w1, cmp_w2, lru_conv_w, lru_conv_b, lru_w_a,
           lru_b_a, lru_w_i, lru_b_i, lru_lam, conv_dw_w, conv_dw_b, conv_ln_g, conv_ln_b, conv_w_pw2,
           w_out, final_g):
    B, T, _ = x.shape
    depth = w_in.shape[0]
    assert T % TK == 0 and T // L_SLC >= K_SEL
    nc = T // S_CMP
    ns = T // L_SLC
    pos_f = positions.astype(F32)
    posc = pos_f[:, :, None]
    posr = pos_f[:, None, :]
    inv_freq = ROPE_THETA ** (-jnp.arange(0, HEAD_DIM, 2, dtype=F32) / HEAD_DIM)
    invn = jnp.tile(inv_freq, LANES // (HEAD_DIM // 2))[None, :]
    invt = inv_freq[:, None]
    ovt = _overlap_t(nc, ns)
    row = lambda v: v[None, :]

    for l in range(depth):
        wn, wt, bg, wtop, wbot, w2, posa, posb, wg = _layer_weights(
            l, w_in, b_gate, cmp_pos, cmp_w1, cmp_w2, lru_w_a, lru_w_i)
        (kvc, ksa, kwa, zn, ul, zl, ga, gb, zc, qt, vst, vwt, gt) = _inproj(
            x, row(norm_g[l]), wn, wt, posc, posr, invn, invt, bg)
        cnat, ctr = _compress(kvc.reshape(2, B, nc, S_CMP * LANES), posa, posb, wtop, wbot, w2)
        gt4 = gt.reshape(B, N_KV, GATE_ROWS, T)
        yct, sb = _cmp_attention(qt, cnat, ctr, gt4, ovt)
        m_nsa = _sel_attention(qt, ksa, vst, kwa, vwt, sb, yct, gt4, zn)
        m_lru = _lru(ul, zl, lru_conv_w[l], row(lru_conv_b[l]), wg, row(lru_b_a[l]), row(lru_b_i[l]),
                     row(lru_lam[l]))
        dw = jnp.concatenate([conv_dw_w[l], jnp.zeros((CONV_HALO - CONV_K, D_GRP), F32)], axis=0)
        m_conv = _conformer(ga, gb, zc, dw, row(conv_dw_b[l]), row(conv_ln_g[l]), row(conv_ln_b[l]),
                            conv_w_pw2[l].astype(BF16))
        x = _outproj(x, m_nsa, m_lru, m_conv, w_out[l].astype(BF16), row(final_g), l == depth - 1)
    return x
```
